```python
import jax, jax.numpy as jnp
from jax import lax
import numpy as np

D_MODEL = 1024
BATCH = 8
SEQ = 2048
DEPTH = 1
DEC_BATCH = 128
DEC_SEQ = 8
PAST_LEN = 16384
PAGE_SIZE = 128

N_MEM = 256
D_A = D_MODEL // 2
CONV_A = 3
D_B = D_MODEL // 2
CONV_B = 31
N_XHEADS = 4
XHEAD_DIM = D_MODEL // 8
D_X = N_XHEADS * XHEAD_DIM
N_BRANCH = 3
D_IN = 3 * D_A + 2 * D_B + D_X + N_BRANCH * D_MODEL
N_EXPERTS = 32
TOP_K = 4
D_FF = D_MODEL
SWIGLU_LIMIT = 7.0
SWIGLU_ALPHA = 1.702
MOE_BLOCK = 128
EPS = 1e-5

kernel_name = 'hybrid_gatedconv_conformer_memxattn_moe_step'


def _rmsnorm(x, g):
    xf = x.astype(jnp.float32)
    y = xf * lax.rsqrt(jnp.mean(xf * xf, axis=-1, keepdims=True) + EPS)
    return (y * g.astype(jnp.float32)).astype(x.dtype)


def _layernorm(x, g, b):
    xf = x.astype(jnp.float32)
    mu = jnp.mean(xf, axis=-1, keepdims=True)
    xc = xf - mu
    var = jnp.mean(xc * xc, axis=-1, keepdims=True)
    return (xc * lax.rsqrt(var + EPS) * g.astype(jnp.float32) + b.astype(jnp.float32)).astype(x.dtype)


def _causal_dwconv(buf, u, w):
    ext = jnp.concatenate([buf.astype(u.dtype), u], axis=1)
    y = lax.conv_general_dilated(ext, w[:, None, :].astype(u.dtype), window_strides=(1,), padding='VALID',
                                 dimension_numbers=('NWC', 'WIO', 'NWC'), feature_group_count=u.shape[-1])
    return y, ext[:, ext.shape[1] - (w.shape[0] - 1):]


def _memory_kv(mem, g, w_kv):
    kv = _rmsnorm(mem, g) @ w_kv
    k, v = jnp.split(kv, 2, axis=-1)
    bm, m, _ = mem.shape
    return k.reshape(bm, m, N_XHEADS, XHEAD_DIM), v.reshape(bm, m, N_XHEADS, XHEAD_DIM)


def _mixer_block(x, buf_a, buf_b, mem_k, mem_v, norm_mix, w_in, w_conv_a, w_out_a, w_conv_b, b_conv_b,
                 ln_g, ln_b, w_out_b, w_out_x, w_o):
    bn, s, _ = x.shape
    h = _rmsnorm(x, norm_mix)
    z = h @ w_in
    cuts = [D_A, 2 * D_A, 3 * D_A, 3 * D_A + D_B, 3 * D_A + 2 * D_B, 3 * D_A + 2 * D_B + D_X]
    a_b, a_c, a_h, b_val, b_gate, q, g = jnp.split(z, cuts, axis=-1)
    conv_a, new_buf_a = _causal_dwconv(buf_a, a_c * a_h, w_conv_a)
    y_a = (a_b * conv_a) @ w_out_a
    conv_b, new_buf_b = _causal_dwconv(buf_b, b_val * jax.nn.sigmoid(b_gate), w_conv_b)
    y_b = jax.nn.silu(_layernorm(conv_b + b_conv_b, ln_g, ln_b)) @ w_out_b
    qh = q.reshape(bn, s, N_XHEADS, XHEAD_DIM)
    sc = jnp.einsum('bshd,bmhd->bhsm', qh, mem_k.astype(qh.dtype), preferred_element_type=jnp.float32) * (XHEAD_DIM ** -0.5)
    p = jax.nn.softmax(sc, axis=-1).astype(mem_v.dtype)
    o = jnp.einsum('bhsm,bmhd->bshd', p, mem_v).reshape(bn, s, D_X)
    y_x = o @ w_out_x
    gates = jax.nn.sigmoid(g.reshape(bn, s, N_BRANCH, D_MODEL))
    merged = gates[:, :, 0] * y_a + gates[:, :, 1] * y_b + gates[:, :, 2] * y_x
    return x + merged @ w_o, new_buf_a, new_buf_b


def _moe(h, w_router, b_router, w_gu, b_gu, w_dn, b_dn):
    t, d = h.shape
    tk = t * TOP_K
    logits = jnp.dot(h, w_router, preferred_element_type=jnp.float32) + b_router.astype(jnp.float32)
    top_v, top_e = lax.top_k(logits, TOP_K)
    gate = jax.nn.softmax(top_v, axis=-1)
    flat_e = top_e.reshape(tk)
    flat_t = jnp.arange(tk, dtype=jnp.int32) // TOP_K
    flat_g = gate.reshape(tk)
    counts = jnp.bincount(flat_e, length=N_EXPERTS)
    padded = (counts + MOE_BLOCK - 1) // MOE_BLOCK * MOE_BLOCK
    pad_end = jnp.cumsum(padded)
    pad_start = pad_end - padded
    start = jnp.cumsum(counts) - counts
    order = jnp.argsort(flat_e)
    e_sorted = flat_e[order]
    dest = pad_start[e_sorted] + jnp.arange(tk, dtype=jnp.int32) - start[e_sorted]
    n_blocks = -(-tk // MOE_BLOCK) + N_EXPERTS
    n_slots = n_blocks * MOE_BLOCK
    slot_tok = jnp.full((n_slots,), t, jnp.int32).at[dest].set(flat_t[order])
    slot_w = jnp.zeros((n_slots,), jnp.float32).at[dest].set(flat_g[order])
    block_e = jnp.minimum(jnp.searchsorted(pad_end, jnp.arange(n_blocks, dtype=jnp.int32) * MOE_BLOCK, side='right'),
                          N_EXPERTS - 1)
    h_pad = jnp.concatenate([h, jnp.zeros((1, d), h.dtype)], axis=0)
    xb = h_pad[slot_tok].reshape(n_blocks, MOE_BLOCK, d)

    def expert_block(args):
        xe, e = args
        gu = xe @ w_gu[e] + b_gu[e]
        gt, up = jnp.split(gu, 2, axis=-1)
        gt = jnp.minimum(gt, SWIGLU_LIMIT)
        up = jnp.clip(up, -SWIGLU_LIMIT, SWIGLU_LIMIT)
        act = gt * jax.nn.sigmoid(SWIGLU_ALPHA * gt) * (up + 1.0)
        return act @ w_dn[e] + b_dn[e]

    out = lax.map(expert_block, (xb, block_e)).reshape(n_slots, d)
    y = jax.ops.segment_sum(out * slot_w[:, None].astype(out.dtype), slot_tok, num_segments=t + 1)
    return y[:t]


def _ffn_block(x, norm_ffn, w_router, b_router, w_gu, b_gu, w_dn, b_dn):
    bn, s, d = x.shape
    h = _rmsnorm(x, norm_ffn).reshape(bn * s, d)
    return x + _moe(h, w_router, b_router, w_gu, b_gu, w_dn, b_dn).reshape(bn, s, d)


def setup_inputs(seed: int = 0) -> dict:
    key = jax.random.key(seed)
    ks = jax.random.split(key, 40)
    D = D_MODEL
    L = DEPTH

    def nrm(k, shape, scale):
        return scale * jax.random.normal(k, shape, jnp.float32)

    return {
        'x_prompt': nrm(ks[0], (BATCH, SEQ, D), 1.0),
        'x_sample': nrm(ks[1], (DEC_BATCH, DEC_SEQ, D), 1.0),
        'mem_prompt': nrm(ks[2], (BATCH, N_MEM, D), 1.0),
        'state_conv_a': nrm(ks[3], (L, DEC_BATCH, CONV_A - 1, D_A), 1.0),
        'state_conv_b': nrm(ks[4], (L, DEC_BATCH, CONV_B - 1, D_B), 0.5),
        'cache_mem_k': nrm(ks[5], (L, DEC_BATCH, N_MEM, N_XHEADS, XHEAD_DIM), 1.0),
        'cache_mem_v': nrm(ks[6], (L, DEC_BATCH, N_MEM, N_XHEADS, XHEAD_DIM), 1.0),
        'norm_mix': 1.0 + nrm(ks[7], (L, D), 0.02),
        'w_in': nrm(ks[8], (L, D, D_IN), D ** -0.5),
        'w_conv_a': nrm(ks[9], (L, CONV_A, D_A), CONV_A ** -0.5),
        'w_out_a': nrm(ks[10], (L, D_A, D), D_A ** -0.5),
        'w_conv_b': nrm(ks[11], (L, CONV_B, D_B), CONV_B ** -0.5),
        'b_conv_b': nrm(ks[12], (L, D_B), 0.02),
        'ln_conv_b_g': 1.0 + nrm(ks[13], (L, D_B), 0.02),
        'ln_conv_b_b': nrm(ks[14], (L, D_B), 0.02),
        'w_out_b': nrm(ks[15], (L, D_B, D), D_B ** -0.5),
        'norm_mem': 1.0 + nrm(ks[16], (L, D), 0.02),
        'w_mem_kv': nrm(ks[17], (L, D, 2 * D_X), D ** -0.5),
        'w_out_x': nrm(ks[18], (L, D_X, D), D_X ** -0.5),
        'w_o': nrm(ks[19], (L, D, D), D ** -0.5),
        'norm_ffn': 1.0 + nrm(ks[20], (L, D), 0.02),
        'w_router': nrm(ks[21], (L, D, N_EXPERTS), D ** -0.5),
        'b_router': nrm(ks[22], (L, N_EXPERTS), 0.01),
        'w_gu': nrm(ks[23], (L, N_EXPERTS, D, 2 * D_FF), D ** -0.5),
        'b_gu': nrm(ks[24], (L, N_EXPERTS, 2 * D_FF), 0.01),
        'w_dn': nrm(ks[25], (L, N_EXPERTS, D_FF, D), D_FF ** -0.5),
        'b_dn': nrm(ks[26], (L, N_EXPERTS, D), 0.01),
        'norm_final': 1.0 + nrm(ks[27], (D,), 0.02),
    }


def reference(x_prompt, x_sample, mem_prompt, state_conv_a, state_conv_b, cache_mem_k, cache_mem_v,
              norm_mix, w_in, w_conv_a, w_out_a, w_conv_b, b_conv_b, ln_conv_b_g, ln_conv_b_b, w_out_b,
              norm_mem, w_mem_kv, w_out_x, w_o, norm_ffn, w_router, b_router, w_gu, b_gu, w_dn, b_dn,
              norm_final):
    yp, ys = x_prompt, x_sample
    bp = x_prompt.shape[0]
    pa, pb, pk, pv, sa, sb = [], [], [], [], [], []
    for l in range(DEPTH):
        mix_w = (norm_mix[l], w_in[l], w_conv_a[l], w_out_a[l], w_conv_b[l], b_conv_b[l],
                 ln_conv_b_g[l], ln_conv_b_b[l], w_out_b[l], w_out_x[l], w_o[l])
        ffn_w = (norm_ffn[l], w_router[l], b_router[l], w_gu[l], b_gu[l], w_dn[l], b_dn[l])
        k_p, v_p = _memory_kv(mem_prompt, norm_mem[l], w_mem_kv[l])
        zero_a = jnp.zeros((bp, CONV_A - 1, D_A), x_prompt.dtype)
        zero_b = jnp.zeros((bp, CONV_B - 1, D_B), x_prompt.dtype)
        yp, na_p, nb_p = _mixer_block(yp, zero_a, zero_b, k_p, v_p, *mix_w)
        yp = _ffn_block(yp, *ffn_w)
        ys, na_s, nb_s = _mixer_block(ys, state_conv_a[l], state_conv_b[l], cache_mem_k[l], cache_mem_v[l], *mix_w)
        ys = _ffn_block(ys, *ffn_w)
        pa.append(na_p)
        pb.append(nb_p)
        pk.append(k_p)
        pv.append(v_p)
        sa.append(na_s)
        sb.append(nb_s)
    y_prompt = _rmsnorm(yp, norm_final)
    y_sample = _rmsnorm(ys, norm_final)
    return (y_prompt, y_sample, jnp.stack(pa), jnp.stack(pb), jnp.stack(pk), jnp.stack(pv), jnp.stack(sa), jnp.stack(sb))
```

```python
import functools

import jax
import jax.numpy as jnp
from jax import lax
from jax.experimental import pallas as pl
from jax.experimental.pallas import tpu as pltpu

F32 = jnp.float32
BF16 = jnp.bfloat16

D_MODEL = 1024
N_MEM = 256
D_A = 512
D_B = 512
D_X = 512
N_XHEADS = 4
XHEAD_DIM = 128
CONV_A = 3
CONV_B = 31
N_BRANCH = 3
N_EXPERTS = 32
TOP_K = 4
D_FF = 1024
SWIGLU_LIMIT = 7.0
SWIGLU_ALPHA = 1.702
EPS = 1e-5

_ZA = (0, 3 * D_A)
_ZB = (_ZA[1], _ZA[1] + 2 * D_B)
_ZQ = (_ZB[1], _ZB[1] + D_X)
_ZG = (_ZQ[1], _ZQ[1] + N_BRANCH * D_MODEL)

VMEM_LIMIT = 56 * 1024 * 1024
SUBLANES = 8

HIST_A = 8
HIST_B = 32

MOE_BM = 256


def _rms(x, g):
    return x * lax.rsqrt(jnp.mean(x * x, axis=-1, keepdims=True) + EPS) * g


def _sigmoid(x):
    return 1.0 / (1.0 + jnp.exp(-x))


def _const_spec(shape):
    nd = len(shape)
    return pl.BlockSpec(shape, lambda *_: (0,) * nd, pipeline_mode=pl.Buffered(1))


def _memkv_kernel(mem_ref, g_ref, w_ref, k_ref, v_ref):
    h = _rms(mem_ref[0], g_ref[...]).astype(BF16)
    kv = jnp.dot(h, w_ref[...], preferred_element_type=F32)
    k_ref[0] = kv[:, :D_X]
    v_ref[0] = kv[:, D_X:]


def _memkv(mem, g, w_bf):
    b = mem.shape[0]
    out = jax.ShapeDtypeStruct((b, N_MEM, D_X), F32)
    return pl.pallas_call(
        _memkv_kernel,
        grid=(b,),
        in_specs=[
            pl.BlockSpec((1, N_MEM, D_MODEL), lambda i: (i, 0, 0)),
            _const_spec((1, D_MODEL)),
            _const_spec((D_MODEL, 2 * D_X)),
        ],
        out_specs=[pl.BlockSpec((1, N_MEM, D_X), lambda i: (i, 0, 0))] * 2,
        out_shape=[out, out],
        compiler_params=pltpu.CompilerParams(dimension_semantics=("arbitrary",), vmem_limit_bytes=VMEM_LIMIT),
        name="memkv",
    )(mem, g, w_bf)


def _mixer_kernel(x_ref, sa_ref, sb_ref, k_ref, v_ref,
                  nmix_ref, win_ref, wca_ref, woa_ref, wcb_ref, bcb_ref, lng_ref, lnb_ref,
                  wob_ref, wox_ref, wo_ref, nffn_ref, wrt_ref, br_ref,
                  x1_ref, nca_ref, ncb_ref, tope_ref, gate_ref,
                  exta, extb, kbf, vbf, za_s, zb_s, acta_s, actb_s,
                  *, nb, ts, nst):
    m = nb * ts
    s = pl.program_id(1)

    @pl.when(s == 0)
    def _():
        exta[:, HIST_A - (CONV_A - 1):HIST_A, :] = sa_ref[...]
        extb[:, HIST_B - (CONV_B - 1):HIST_B, :] = sb_ref[...]
        kbf[...] = k_ref[...].astype(BF16)
        vbf[...] = v_ref[...].astype(BF16)

    if nst > 1:
        @pl.when(s > 0)
        def _():
            exta[:, 0:HIST_A, :] = exta[:, ts:ts + HIST_A, :]
            extb[:, 0:HIST_B, :] = extb[:, ts:ts + HIST_B, :]

    x = x_ref[...].reshape(m, D_MODEL)
    h = _rms(x, nmix_ref[...]).astype(BF16)

    za_s[...] = jnp.dot(h, win_ref[:, _ZA[0]:_ZA[1]], preferred_element_type=F32)
    zb_s[...] = jnp.dot(h, win_ref[:, _ZB[0]:_ZB[1]], preferred_element_type=F32)

    u_a = za_s[:, D_A:2 * D_A] * za_s[:, 2 * D_A:3 * D_A]
    exta[:, HIST_A:HIST_A + ts, :] = u_a.reshape(nb, ts, D_A)
    u_b = zb_s[:, 0:D_B] * _sigmoid(zb_s[:, D_B:2 * D_B])
    extb[:, HIST_B:HIST_B + ts, :] = u_b.reshape(nb, ts, D_B)
    nca_ref[...] = exta[:, HIST_A + ts - (CONV_A - 1):HIST_A + ts, :]
    ncb_ref[...] = extb[:, HIST_B + ts - (CONV_B - 1):HIST_B + ts, :]

    rc = min(ts, 32)
    cps = ts // rc

    for ci in range(nb * cps):
        n, c = ci // cps, ci % cps
        r0 = c * rc
        row0 = n * ts + r0
        acc = None
        for kk in range(CONV_A):
            t = wca_ref[kk:kk + 1, :] * exta[n, pl.ds(r0 + HIST_A - (CONV_A - 1) + kk, rc), :]
            acc = t if acc is None else acc + t
        acta_s[pl.ds(row0, rc), :] = za_s[pl.ds(row0, rc), 0:D_A] * acc
        acc = None
        for kk in range(CONV_B):
            t = wcb_ref[kk:kk + 1, :] * extb[n, pl.ds(r0 + HIST_B - (CONV_B - 1) + kk, rc), :]
            acc = t if acc is None else acc + t
        cb = acc + bcb_ref[...]
        mu = jnp.mean(cb, axis=-1, keepdims=True)
        xc = cb - mu
        var = jnp.mean(xc * xc, axis=-1, keepdims=True)
        ln = xc * lax.rsqrt(var + EPS) * lng_ref[...] + lnb_ref[...]
        actb_s[pl.ds(row0, rc), :] = ln * _sigmoid(ln)

    y_a = jnp.dot(acta_s[...].astype(BF16), woa_ref[...], preferred_element_type=F32)
    y_b = jnp.dot(actb_s[...].astype(BF16), wob_ref[...], preferred_element_type=F32)

    q = jnp.dot(h, win_ref[:, _ZQ[0]:_ZQ[1]], preferred_element_type=F32)
    scale = XHEAD_DIM ** -0.5
    rows = []
    for n in range(nb):
        heads = []
        for hd in range(N_XHEADS):
            lo, hi = hd * XHEAD_DIM, (hd + 1) * XHEAD_DIM
            qh = q[n * ts:(n + 1) * ts, lo:hi].astype(BF16)
            sc = lax.dot_general(qh, kbf[n, :, lo:hi], (((1,), (1,)), ((), ())),
                                 preferred_element_type=F32) * scale
            e = jnp.exp(sc - jnp.max(sc, axis=-1, keepdims=True))
            p = (e / jnp.sum(e, axis=-1, keepdims=True)).astype(BF16)
            heads.append(jnp.dot(p, vbf[n, :, lo:hi], preferred_element_type=F32))
        rows.append(jnp.concatenate(heads, axis=1))
    o = rows[0] if nb == 1 else jnp.concatenate(rows, axis=0)
    y_x = jnp.dot(o.astype(BF16), wox_ref[...], preferred_element_type=F32)

    g = jnp.dot(h, win_ref[:, _ZG[0]:_ZG[1]], preferred_element_type=F32)
    merged = (_sigmoid(g[:, 0:D_MODEL]) * y_a
              + _sigmoid(g[:, D_MODEL:2 * D_MODEL]) * y_b
              + _sigmoid(g[:, 2 * D_MODEL:3 * D_MODEL]) * y_x)
    x1 = x + jnp.dot(merged.astype(BF16), wo_ref[...], preferred_element_type=F32)
    x1_ref[...] = x1.reshape(nb, ts, D_MODEL)

    h2 = _rms(x1, nffn_ref[...]).astype(BF16)
    logits = lax.dot_general(wrt_ref[...], h2, (((1,), (1,)), ((), ())),
                             preferred_element_type=F32) + br_ref[...]
    eidx = lax.broadcasted_iota(jnp.int32, logits.shape, 0)
    vals = logits
    top_v, top_i = [], []
    for _ in range(TOP_K):
        mx = jnp.max(vals, axis=0, keepdims=True)
        ix = jnp.min(jnp.where(vals == mx, eidx, N_EXPERTS), axis=0, keepdims=True)
        top_v.append(mx)
        top_i.append(ix)
        vals = jnp.where(eidx == ix, -jnp.inf, vals)
    ex = [jnp.exp(v - top_v[0]) for v in top_v]
    den = ex[0] + ex[1] + ex[2] + ex[3]
    tope_ref[0] = jnp.concatenate(top_i, axis=0)
    gate_ref[0] = jnp.concatenate([e / den for e in ex], axis=0)


def _mixer(x, sa, sb, k, v, w, *, nb, ts):
    b, sq, _ = x.shape
    nbt, nst = b // nb, sq // ts
    m = nb * ts
    kern = functools.partial(_mixer_kernel, nb=nb, ts=ts, nst=nst)
    in_specs = [
        pl.BlockSpec((nb, ts, D_MODEL), lambda i, j: (i, j, 0)),
        pl.BlockSpec((nb, CONV_A - 1, D_A), lambda i, j: (i, 0, 0)),
        pl.BlockSpec((nb, CONV_B - 1, D_B), lambda i, j: (i, 0, 0)),
        pl.BlockSpec((nb, N_MEM, D_X), lambda i, j: (i, 0, 0)),
        pl.BlockSpec((nb, N_MEM, D_X), lambda i, j: (i, 0, 0)),
    ] + [_const_spec(a.shape) for a in w]
    out_shape = [
        jax.ShapeDtypeStruct((b, sq, D_MODEL), F32),
        jax.ShapeDtypeStruct((b, CONV_A - 1, D_A), F32),
        jax.ShapeDtypeStruct((b, CONV_B - 1, D_B), F32),
        jax.ShapeDtypeStruct((nbt * nst, TOP_K, m), jnp.int32),
        jax.ShapeDtypeStruct((nbt * nst, TOP_K, m), F32),
    ]
    out_specs = [
        pl.BlockSpec((nb, ts, D_MODEL), lambda i, j: (i, j, 0)),
        pl.BlockSpec((nb, CONV_A - 1, D_A), lambda i, j: (i, 0, 0)),
        pl.BlockSpec((nb, CONV_B - 1, D_B), lambda i, j: (i, 0, 0)),
        pl.BlockSpec((1, TOP_K, m), lambda i, j: (i * nst + j, 0, 0)),
        pl.BlockSpec((1, TOP_K, m), lambda i, j: (i * nst + j, 0, 0)),
    ]
    scratch = [
        pltpu.VMEM((nb, HIST_A + ts, D_A), F32),
        pltpu.VMEM((nb, HIST_B + ts, D_B), F32),
        pltpu.VMEM((nb, N_MEM, D_X), BF16),
        pltpu.VMEM((nb, N_MEM, D_X), BF16),
        pltpu.VMEM((m, 3 * D_A), F32),
        pltpu.VMEM((m, 2 * D_B), F32),
        pltpu.VMEM((m, D_A), F32),
        pltpu.VMEM((m, D_B), F32),
    ]
    return pl.pallas_call(
        kern,
        grid=(nbt, nst),
        in_specs=in_specs,
        out_specs=out_specs,
        out_shape=out_shape,
        scratch_shapes=scratch,
        compiler_params=pltpu.CompilerParams(dimension_semantics=("arbitrary", "arbitrary"),
                                             vmem_limit_bytes=VMEM_LIMIT),
        name=f"mixer_nb{nb}_ts{ts}",
    )(x, sa, sb, k, v, *w)


def _routing_plan(top_e, tp, tsm, bm, nblk):
    t = tp + tsm
    tk = t * TOP_K
    flat = jnp.arange(tk, dtype=jnp.int32)
    keys = jnp.sort(top_e.reshape(tk) * (1 << 17) + flat)
    counts = jnp.sum((top_e.reshape(tk, 1) == jnp.arange(N_EXPERTS, dtype=jnp.int32)[None, :]).astype(jnp.int32),
                     axis=0)
    start = jnp.cumsum(counts) - counts
    nbe = (counts + bm - 1) // bm
    blk_end = jnp.cumsum(nbe)
    blk_start = blk_end - nbe
    nused = blk_end[-1]
    bidx = jnp.arange(nblk, dtype=jnp.int32)
    block_e = jnp.minimum(jnp.sum((bidx[:, None] >= blk_end[None, :]).astype(jnp.int32), axis=1), N_EXPERTS - 1)
    kb = bidx - blk_start[block_e]
    pos0 = start[block_e] + kb * bm
    nvalid = jnp.clip(counts[block_e] - kb * bm, 0, bm)
    kk = jnp.arange(bm, dtype=jnp.int32)
    valid = kk[None, :] < nvalid[:, None]
    fl = keys[jnp.minimum(pos0[:, None] + kk[None, :], tk - 1)] & ((1 << 17) - 1)
    tok = fl >> 2
    jj = fl & 3
    is_p = valid & (tok < tp)
    nprompt = jnp.sum(is_p.astype(jnp.int32), axis=1)
    src = jnp.where(valid, jnp.where(is_p, tok, tok - tp), 0)
    dst = jnp.where(valid, jnp.where(is_p, jj * tp + tok, jj * tsm + (tok - tp)), 0)
    meta = jnp.stack([block_e, nprompt, nvalid], axis=0)
    return meta, nused.reshape(1), src.reshape(nblk, 1, bm), dst.reshape(nblk, 1, bm)


def _moe_kernel(meta_ref, nused_ref, src_ref, srcn_ref, dst_ref, xp_hbm, xs_hbm, nffn_ref,
                wgu_ref, bgu_ref, wdn_ref, bdn_ref,
                yp_hbm, ys_hbm,
                xbuf, obuf, wgu_bf, wdn_bf, gsem, ssem, *, bm):
    i = pl.program_id(0)
    nused = nused_ref[0]
    slot = lax.rem(i, 2)

    def start_gather(idx_ref, blk, sl):
        npr = meta_ref[1, blk]

        def one(src_hbm):
            def body(r, c):
                pltpu.make_async_copy(src_hbm.at[pl.ds(idx_ref[0, 0, r], 1)],
                                      xbuf.at[sl, pl.ds(r, 1)], gsem.at[sl]).start()
                return c
            return body

        lax.fori_loop(0, npr, one(xp_hbm), 0)
        lax.fori_loop(npr, bm, one(xs_hbm), 0)

    def wait_gather(sl):
        pltpu.make_async_copy(xp_hbm.at[pl.ds(0, bm)], xbuf.at[sl], gsem.at[sl]).wait()

    def start_scatter(blk, sl):
        npr = meta_ref[1, blk]
        nv = meta_ref[2, blk]

        def one(dst_hbm):
            def body(r, c):
                pltpu.make_async_copy(obuf.at[sl, pl.ds(r, 1)],
                                      dst_hbm.at[pl.ds(dst_ref[0, 0, r], 1)], ssem.at[sl]).start()
                return c
            return body

        lax.fori_loop(0, npr, one(yp_hbm), 0)
        lax.fori_loop(npr, nv, one(ys_hbm), 0)

    def wait_scatter(blk, sl):
        def body(r, c):
            pltpu.make_async_copy(obuf.at[sl, pl.ds(0, 1)], yp_hbm.at[pl.ds(0, 1)], ssem.at[sl]).wait()
            return c

        lax.fori_loop(0, meta_ref[2, blk], body, 0)

    @pl.when(i == 0)
    def _():
        start_gather(src_ref, 0, 0)

    @pl.when(i + 1 < nused)
    def _():
        start_gather(srcn_ref, i + 1, 1 - slot)

    @pl.when(i < nused)
    def _():
        wait_gather(slot)
        e = meta_ref[0, i]
        e_prev = meta_ref[0, jnp.maximum(i - 1, 0)]

        @pl.when((i == 0) | (e != e_prev))
        def _():
            wgu_bf[...] = wgu_ref[0].astype(BF16)
            wdn_bf[...] = wdn_ref[0].astype(BF16)

        h = _rms(xbuf[slot], nffn_ref[...]).astype(BF16)
        gu = jnp.dot(h, wgu_bf[...], preferred_element_type=F32) + bgu_ref[0]
        gt = jnp.minimum(gu[:, :D_FF], SWIGLU_LIMIT)
        up = jnp.clip(gu[:, D_FF:], -SWIGLU_LIMIT, SWIGLU_LIMIT)
        act = (gt * _sigmoid(SWIGLU_ALPHA * gt) * (up + 1.0)).astype(BF16)
        out = jnp.dot(act, wdn_bf[...], preferred_element_type=F32) + bdn_ref[0]

        @pl.when(i >= 2)
        def _():
            wait_scatter(i - 2, slot)

        obuf[slot] = out
        start_scatter(i, slot)

        @pl.when(i == nused - 1)
        def _():
            @pl.when(i >= 1)
            def _():
                wait_scatter(i - 1, 1 - slot)

            wait_scatter(i, slot)


def _moe(meta, nused, src, dst, x1p, x1s, nffn, w_gu, b_gu, w_dn, b_dn, *, bm, nblk):
    tp, tsm = x1p.shape[0], x1s.shape[0]
    kern = functools.partial(_moe_kernel, bm=bm)
    smem_blk = lambda f: pl.BlockSpec((1, 1, bm), f, memory_space=pltpu.SMEM)
    grid_spec = pltpu.PrefetchScalarGridSpec(
        num_scalar_prefetch=2,
        grid=(nblk,),
        in_specs=[
            smem_blk(lambda i, mt, nu: (i, 0, 0)),
            smem_blk(lambda i, mt, nu: (jnp.minimum(i + 1, nblk - 1), 0, 0)),
            smem_blk(lambda i, mt, nu: (i, 0, 0)),
            pl.BlockSpec(memory_space=pl.ANY),
            pl.BlockSpec(memory_space=pl.ANY),
            pl.BlockSpec((1, D_MODEL), lambda i, mt, nu: (0, 0)),
            pl.BlockSpec((1, D_MODEL, 2 * D_FF), lambda i, mt, nu: (mt[0, i], 0, 0)),
            pl.BlockSpec((1, 1, 2 * D_FF), lambda i, mt, nu: (mt[0, i], 0, 0)),
            pl.BlockSpec((1, D_FF, D_MODEL), lambda i, mt, nu: (mt[0, i], 0, 0)),
            pl.BlockSpec((1, 1, D_MODEL), lambda i, mt, nu: (mt[0, i], 0, 0)),
        ],
        out_specs=[pl.BlockSpec(memory_space=pl.ANY), pl.BlockSpec(memory_space=pl.ANY)],
        scratch_shapes=[
            pltpu.VMEM((2, bm, D_MODEL), F32),
            pltpu.VMEM((2, bm, D_MODEL), F32),
            pltpu.VMEM((D_MODEL, 2 * D_FF), BF16),
            pltpu.VMEM((D_FF, D_MODEL), BF16),
            pltpu.SemaphoreType.DMA((2,)),
            pltpu.SemaphoreType.DMA((2,)),
        ],
    )
    return pl.pallas_call(
        kern,
        grid_spec=grid_spec,
        out_shape=[jax.ShapeDtypeStruct((TOP_K * tp, D_MODEL), F32),
                   jax.ShapeDtypeStruct((TOP_K * tsm, D_MODEL), F32)],
        compiler_params=pltpu.CompilerParams(dimension_semantics=("arbitrary",), vmem_limit_bytes=VMEM_LIMIT),
        name="moe_experts",
    )(meta, nused, src, src, dst, x1p, x1s, nffn, w_gu, b_gu, w_dn, b_dn)


def _combine_kernel(x1_ref, y4_ref, gate_ref, nf_ref, o_ref):
    g = gate_ref[...]
    y = x1_ref[...]
    for j in range(TOP_K):
        y = y + g[:, j:j + 1] * y4_ref[j]
    o_ref[...] = _rms(y, nf_ref[...])


def _combine(x1, y4, gate, nf, *, tm):
    t = x1.shape[0]
    return pl.pallas_call(
        _combine_kernel,
        grid=(t // tm,),
        in_specs=[
            pl.BlockSpec((tm, D_MODEL), lambda i: (i, 0)),
            pl.BlockSpec((TOP_K, tm, D_MODEL), lambda i: (0, i, 0)),
            pl.BlockSpec((tm, TOP_K), lambda i: (i, 0)),
            pl.BlockSpec((1, D_MODEL), lambda i: (0, 0)),
        ],
        out_specs=pl.BlockSpec((tm, D_MODEL), lambda i: (i, 0)),
        out_shape=jax.ShapeDtypeStruct((t, D_MODEL), F32),
        compiler_params=pltpu.CompilerParams(dimension_semantics=("arbitrary",), vmem_limit_bytes=VMEM_LIMIT),
        name="combine",
    )(x1, y4, gate, nf)


def _layer(xp, xs, mem, sa, sb, ck, cv, norm_mix, w_in, w_conv_a, w_out_a, w_conv_b, b_conv_b, ln_g, ln_b,
           w_out_b, norm_mem, w_mem_kv, w_out_x, w_o, norm_ffn, w_router, b_router, w_gu, b_gu, w_dn, b_dn):
    bp, sp, _ = xp.shape
    bs, ss, _ = xs.shape
    tp, tsm = bp * sp, bs * ss
    row = lambda a: a.reshape(1, -1)

    kp, vp = _memkv(mem, row(norm_mem), w_mem_kv.astype(BF16))

    w = (row(norm_mix), w_in.astype(BF16), w_conv_a, w_out_a.astype(BF16), w_conv_b, row(b_conv_b),
         row(ln_g), row(ln_b), w_out_b.astype(BF16), w_out_x.astype(BF16), w_o.astype(BF16),
         row(norm_ffn), w_router.T.astype(BF16), b_router.reshape(N_EXPERTS, 1))

    zero_a = jnp.zeros((bp, CONV_A - 1, D_A), F32)
    zero_b = jnp.zeros((bp, CONV_B - 1, D_B), F32)
    x1p, nap, nbp, tep, gp = _mixer(xp, zero_a, zero_b, kp, vp, w, nb=1, ts=256)
    x1s, nas, nbs, tes, gs = _mixer(xs, sa, sb, ck.reshape(bs, N_MEM, D_X), cv.reshape(bs, N_MEM, D_X), w,
                                    nb=8, ts=ss)

    tok_major = lambda a, t: jnp.transpose(a, (0, 2, 1)).reshape(t, TOP_K)
    top_e = jnp.concatenate([tok_major(tep, tp), tok_major(tes, tsm)], axis=0)
    gate_p, gate_s = tok_major(gp, tp), tok_major(gs, tsm)

    bm = MOE_BM
    nblk = -(-(tp + tsm) * TOP_K // bm) + N_EXPERTS
    meta, nused, src, dst = _routing_plan(top_e, tp, tsm, bm, nblk)
    y4p, y4s = _moe(meta, nused, src, dst, x1p.reshape(tp, D_MODEL), x1s.reshape(tsm, D_MODEL), row(norm_ffn),
                    w_gu, b_gu.reshape(N_EXPERTS, 1, 2 * D_FF), w_dn, b_dn.reshape(N_EXPERTS, 1, D_MODEL),
                    bm=bm, nblk=nblk)
    return (x1p.reshape(tp, D_MODEL), y4p.reshape(TOP_K, tp, D_MODEL), gate_p,
            x1s.reshape(tsm, D_MODEL), y4s.reshape(TOP_K, tsm, D_MODEL), gate_s,
            nap, nbp, kp, vp, nas, nbs)


def kernel(x_prompt, x_sample, mem_prompt, state_conv_a, state_conv_b, cache_mem_k, cache_mem_v, norm_mix, w_in, w_conv_a, w_out_a, w_conv_b, b_conv_b, ln_conv_b_g, ln_conv_b_b, w_out_b, norm_mem, w_mem_kv, w_out_x, w_o, norm_ffn, w_router, b_router, w_gu, b_gu, w_dn, b_dn, norm_final):
    depth = norm_mix.shape[0]
    assert depth == 1, "single trunk layer"
    bp, sp, _ = x_prompt.shape
    bs, ss, _ = x_sample.shape
    l = 0
    (x1p, y4p, gate_p, x1s, y4s, gate_s, nap, nbp, kp, vp, nas, nbs) = _layer(
        x_prompt, x_sample, mem_prompt, state_conv_a[l], state_conv_b[l], cache_mem_k[l], cache_mem_v[l],
        norm_mix[l], w_in[l], w_conv_a[l], w_out_a[l], w_conv_b[l], b_conv_b[l], ln_conv_b_g[l], ln_conv_b_b[l],
        w_out_b[l], norm_mem[l], w_mem_kv[l], w_out_x[l], w_o[l], norm_ffn[l], w_router[l], b_router[l],
        w_gu[l], b_gu[l], w_dn[l], b_dn[l])
    nf = norm_final.reshape(1, D_MODEL)
    y_prompt = _combine(x1p, y4p, gate_p, nf, tm=256).reshape(bp, sp, D_MODEL)
    y_sample = _combine(x1s, y4s, gate_s, nf, tm=256).reshape(bs, ss, D_MODEL)
    kv_shape = (1, bp, N_MEM, N_XHEADS, XHEAD_DIM)
    return (y_prompt, y_sample, nap[None], nbp[None], kp.reshape(kv_shape), vp.reshape(kv_shape),
            nas[None], nbs[None])
```
